```python
import math, functools
import jax, jax.numpy as jnp
from jax import lax
import numpy as np

D_MODEL = 1024
BATCH = 8
SEQ = 2048
DEPTH = 2
DEC_BATCH = 32
DEC_SEQ = 1
PAST_LEN = 16384
PAGE_SIZE = 128

ATT_WIDTH = D_MODEL // 2
N_ATT_HEADS = 4
V_HEAD_DIM = ATT_WIDTH // N_ATT_HEADS
SUB_DIM = V_HEAD_DIM // 2
ROPE_DIM = SUB_DIM // 4
ROPE_THETA = 500000.0
LRU_WIDTH = D_MODEL - ATT_WIDTH
N_LRU_BLOCKS = 8
LRU_BLOCK = LRU_WIDTH // N_LRU_BLOCKS
CONV_W = 4
RG_C = 8.0
MIX_WIDTH = ATT_WIDTH + LRU_WIDTH
IN_WIDTH = 3 * ATT_WIDTH + 2 * LRU_WIDTH
N_GROUPS = 4
EXPERTS_PER_GROUP = 4
N_EXPERTS = N_GROUPS * EXPERTS_PER_GROUP
TOP_K_IN_GROUP = 2
D_EXPERT = D_MODEL // 2
PLE_DIM = 256
Q_BLOCK = 128
EPS = 1e-6
NEG = -1e30

kernel_name = "hymba_diffattn_rglru_hiermoe_step"


def rms_norm(x, g):
    xf = x.astype(jnp.float32)
    y = xf * lax.rsqrt(jnp.mean(xf * xf, axis=-1, keepdims=True) + EPS)
    return (y * g.astype(jnp.float32)).astype(x.dtype)


def rope_partial(x, pos):
    half = ROPE_DIM // 2
    inv = ROPE_THETA ** (-jnp.arange(half, dtype=jnp.float32) * 2.0 / ROPE_DIM)
    ang = pos.astype(jnp.float32)[:, None] * inv[None, :]
    cos = jnp.cos(ang)[:, None, None, :]
    sin = jnp.sin(ang)[:, None, None, :]
    xr = x[..., :ROPE_DIM].astype(jnp.float32)
    x1, x2 = xr[..., :half], xr[..., half:]
    rot = jnp.concatenate([x1 * cos - x2 * sin, x2 * cos + x1 * sin], axis=-1).astype(x.dtype)
    return jnp.concatenate([rot, x[..., ROPE_DIM:]], axis=-1)


def diff_softmax_combine(q, keys, vals, masks, lam):
    scale = SUB_DIM ** -0.5
    scores = [jnp.where(m[None, None, None],
                        jnp.einsum('bqhcd,bkhcd->bhcqk', q, k).astype(jnp.float32) * scale, NEG)
              for k, m in zip(keys, masks)]
    p = jax.nn.softmax(jnp.concatenate(scores, axis=-1), axis=-1)
    attn = p[:, :, 0] - lam * p[:, :, 1]
    out = None
    off = 0
    for v in vals:
        n = v.shape[1]
        part = jnp.einsum('bhqk,bkhd->bqhd', attn[..., off:off + n].astype(v.dtype), v)
        out = part if out is None else out + part
        off += n
    return out


def prompt_attention(q, k, v, lam):
    b, s = q.shape[:2]
    nb = s // Q_BLOCK
    qb = q.reshape(b, nb, Q_BLOCK, N_ATT_HEADS, 2, SUB_DIM).transpose(1, 0, 2, 3, 4, 5)
    kpos = jnp.arange(s)

    def one_block(args):
        qi, blk = args
        qpos = blk * Q_BLOCK + jnp.arange(Q_BLOCK)
        mask = kpos[None, :] <= qpos[:, None]
        return diff_softmax_combine(qi, [k], [v], [mask], lam)

    out = lax.map(one_block, (qb, jnp.arange(nb)))
    return out.transpose(1, 0, 2, 3, 4).reshape(b, s, N_ATT_HEADS, V_HEAD_DIM)


def sample_attention(q, k, v, lam, past_k, past_v):
    nq = q.shape[1]
    m_past = jnp.ones((nq, past_k.shape[1]), dtype=bool)
    m_new = jnp.arange(nq)[None, :] <= jnp.arange(nq)[:, None]
    return diff_softmax_combine(q, [past_k, k], [past_v, v], [m_past, m_new], lam)


def causal_conv(u, buf, w, bias):
    s = u.shape[1]
    xp = jnp.concatenate([buf.astype(u.dtype), u], axis=1)
    out = bias
    for j in range(CONV_W):
        out = out + xp[:, j:j + s] * w[j]
    return out, xp[:, -(CONV_W - 1):]


def rg_lru(xc, h0, w_a, b_a, w_x, b_x, lru_lambda):
    b, s, c = xc.shape
    xb = xc.reshape(b, s, N_LRU_BLOCKS, LRU_BLOCK)
    r = jax.nn.sigmoid(jnp.einsum('bsnc,ncd->bsnd', xb, w_a).reshape(b, s, c) + b_a)
    i = jax.nn.sigmoid(jnp.einsum('bsnc,ncd->bsnd', xb, w_x).reshape(b, s, c) + b_x)
    log_a = -RG_C * r.astype(jnp.float32) * jax.nn.softplus(-lru_lambda.astype(jnp.float32))
    a = jnp.exp(log_a)
    u = jnp.sqrt(-jnp.expm1(2.0 * log_a)) * (i * xc).astype(jnp.float32)

    def step(h, au):
        a_t, u_t = au
        h = a_t * h + u_t
        return h, h

    h_last, hs = lax.scan(step, h0.astype(jnp.float32), (a.transpose(1, 0, 2), u.transpose(1, 0, 2)))
    return hs.transpose(1, 0, 2).astype(xc.dtype), h_last.astype(xc.dtype)


def hier_moe(h, w_rg, b_rg, w_re, b_re, w_e_gate, w_e_up, w_e_down):
    shp = h.shape
    t = h.reshape(-1, shp[-1])
    n = t.shape[0]
    gl = (t @ w_rg + b_rg).astype(jnp.float32)
    gp = jax.nn.softmax(gl, axis=-1)
    g_idx = jnp.argmax(gl, axis=-1)
    g_w = jnp.take_along_axis(gp, g_idx[:, None], axis=-1)
    el = (t @ w_re + b_re).astype(jnp.float32).reshape(n, N_GROUPS, EXPERTS_PER_GROUP)
    el_sel = jnp.take_along_axis(el, g_idx[:, None, None], axis=1)[:, 0]
    ep = jax.nn.softmax(el_sel, axis=-1)
    top_w, top_i = lax.top_k(ep, TOP_K_IN_GROUP)
    top_w = top_w / jnp.sum(top_w, axis=-1, keepdims=True)
    gate = g_w * top_w
    eid = g_idx[:, None] * EXPERTS_PER_GROUP + top_i
    combine = jnp.sum(jax.nn.one_hot(eid, N_EXPERTS, dtype=jnp.float32) * gate[..., None], axis=1)
    y = jnp.zeros((n, shp[-1]), jnp.float32)
    for e in range(N_EXPERTS):
        he = jax.nn.silu(t @ w_e_gate[e]) * (t @ w_e_up[e])
        y = y + combine[:, e:e + 1] * (he @ w_e_down[e]).astype(jnp.float32)
    return y.astype(h.dtype).reshape(shp)


def trunk_layer(x, ple, pos, conv_buf, h0, attend, w, lam_init):
    b, s, _ = x.shape
    h = rms_norm(x, w['g_mix'])
    z = h @ w['w_in']
    zq, zk, zv, zu, zg = jnp.split(
        z, [ATT_WIDTH, 2 * ATT_WIDTH, 3 * ATT_WIDTH, 3 * ATT_WIDTH + LRU_WIDTH], axis=-1)
    q = rope_partial(rms_norm(zq.reshape(b, s, N_ATT_HEADS, 2, SUB_DIM), w['g_q']), pos)
    k = rope_partial(rms_norm(zk.reshape(b, s, N_ATT_HEADS, 2, SUB_DIM), w['g_k']), pos)
    v = zv.reshape(b, s, N_ATT_HEADS, V_HEAD_DIM)
    f32 = jnp.float32
    lam = (jnp.exp(jnp.sum(w['lam_q1'].astype(f32) * w['lam_k1'].astype(f32)))
           - jnp.exp(jnp.sum(w['lam_q2'].astype(f32) * w['lam_k2'].astype(f32))) + lam_init)
    o = attend(q, k, v, lam)
    o = rms_norm(o, w['g_sub']) * (1.0 - lam_init)
    uc, conv_new = causal_conv(zu, conv_buf, w['conv_w'], w['conv_b'])
    r, h_last = rg_lru(uc, h0, w['w_a'], w['b_a'], w['w_x'], w['b_x'], w['lru_lambda'])
    r = r * jax.nn.gelu(zg)
    mix = jnp.concatenate([o.reshape(b, s, ATT_WIDTH), r], axis=-1)
    x = x + mix @ w['w_out']
    x = x + hier_moe(rms_norm(x, w['g_ffn']), w['w_rg'], w['b_rg'], w['w_re'], w['b_re'],
                     w['w_e_gate'], w['w_e_up'], w['w_e_down'])
    x = x + jax.nn.sigmoid(x @ w['w_pg']) * (ple @ w['w_ple'])
    return x, k.reshape(b, s, N_ATT_HEADS, 2 * SUB_DIM), v, conv_new, h_last


def setup_inputs(seed: int = 0) -> dict:
    key = jax.random.key(seed)
    ks = jax.random.split(key, 40)
    f32 = jnp.float32
    n_pages = PAST_LEN // PAGE_SIZE
    n_used = DEC_BATCH * n_pages
    n_pool = n_used + (n_used + 3) // 4

    def nrm(k, shape, scale):
        return jax.random.normal(k, shape, f32) * scale

    perm = jax.random.permutation(ks[0], n_pool)[:n_used]
    page_table = perm.reshape(DEC_BATCH, n_pages).astype(jnp.int32)
    a_c = jax.random.uniform(ks[1], (DEPTH, LRU_WIDTH), f32, 0.9, 0.999)
    a_base = a_c ** (1.0 / RG_C)
    lru_lambda = jnp.log(a_base) - jnp.log1p(-a_base)
    return {
        'x_prompt': nrm(ks[2], (BATCH, SEQ, D_MODEL), 1.0),
        'x_sample': nrm(ks[3], (DEC_BATCH, DEC_SEQ, D_MODEL), 1.0),
        'cache_k': nrm(ks[4], (DEPTH, n_pool, PAGE_SIZE, N_ATT_HEADS, 2 * SUB_DIM), 1.0),
        'cache_v': nrm(ks[5], (DEPTH, n_pool, PAGE_SIZE, N_ATT_HEADS, V_HEAD_DIM), 1.0),
        'state_conv': nrm(ks[6], (DEPTH, DEC_BATCH, CONV_W - 1, LRU_WIDTH), 1.0),
        'state_h': nrm(ks[7], (DEPTH, DEC_BATCH, LRU_WIDTH), 0.5),
        'page_table': page_table,
        'p_prompt': nrm(ks[8], (DEPTH, BATCH, SEQ, PLE_DIM), 1.0),
        'p_sample': nrm(ks[9], (DEPTH, DEC_BATCH, DEC_SEQ, PLE_DIM), 1.0),
        'g_mix': 1.0 + nrm(ks[10], (DEPTH, D_MODEL), 0.02),
        'w_in': nrm(ks[11], (DEPTH, D_MODEL, IN_WIDTH), D_MODEL ** -0.5),
        'g_q': 1.0 + nrm(ks[12], (DEPTH, SUB_DIM), 0.02),
        'g_k': 1.0 + nrm(ks[13], (DEPTH, SUB_DIM), 0.02),
        'lam_q1': nrm(ks[14], (DEPTH, SUB_DIM), 0.1),
        'lam_k1': nrm(ks[15], (DEPTH, SUB_DIM), 0.1),
        'lam_q2': nrm(ks[16], (DEPTH, SUB_DIM), 0.1),
        'lam_k2': nrm(ks[17], (DEPTH, SUB_DIM), 0.1),
        'g_sub': 1.0 + nrm(ks[18], (DEPTH, V_HEAD_DIM), 0.02),
        'conv_w': nrm(ks[19], (DEPTH, CONV_W, LRU_WIDTH), CONV_W ** -0.5),
        'conv_b': nrm(ks[20], (DEPTH, LRU_WIDTH), 0.01),
        'w_a': nrm(ks[21], (DEPTH, N_LRU_BLOCKS, LRU_BLOCK, LRU_BLOCK), LRU_BLOCK ** -0.5),
        'b_a': nrm(ks[22], (DEPTH, LRU_WIDTH), 0.01),
        'w_x': nrm(ks[23], (DEPTH, N_LRU_BLOCKS, LRU_BLOCK, LRU_BLOCK), LRU_BLOCK ** -0.5),
        'b_x': nrm(ks[24], (DEPTH, LRU_WIDTH), 0.01),
        'lru_lambda': lru_lambda,
        'w_out': nrm(ks[25], (DEPTH, MIX_WIDTH, D_MODEL), MIX_WIDTH ** -0.5),
        'g_ffn': 1.0 + nrm(ks[26], (DEPTH, D_MODEL), 0.02),
        'w_rg': nrm(ks[27], (DEPTH, D_MODEL, N_GROUPS), D_MODEL ** -0.5),
        'b_rg': nrm(ks[28], (DEPTH, N_GROUPS), 0.01),
        'w_re': nrm(ks[29], (DEPTH, D_MODEL, N_EXPERTS), D_MODEL ** -0.5),
        'b_re': nrm(ks[30], (DEPTH, N_EXPERTS), 0.01),
        'w_e_gate': nrm(ks[31], (DEPTH, N_EXPERTS, D_MODEL, D_EXPERT), D_MODEL ** -0.5),
        'w_e_up': nrm(ks[32], (DEPTH, N_EXPERTS, D_MODEL, D_EXPERT), D_MODEL ** -0.5),
        'w_e_down': nrm(ks[33], (DEPTH, N_EXPERTS, D_EXPERT, D_MODEL), D_EXPERT ** -0.5),
        'w_ple': nrm(ks[34], (DEPTH, PLE_DIM, D_MODEL), PLE_DIM ** -0.5),
        'w_pg': nrm(ks[35], (DEPTH, D_MODEL, D_MODEL), D_MODEL ** -0.5),
    }


def reference(x_prompt, x_sample, cache_k, cache_v, state_conv, state_h, page_table,
              p_prompt, p_sample, g_mix, w_in, g_q, g_k, lam_q1, lam_k1, lam_q2, lam_k2,
              g_sub, conv_w, conv_b, w_a, b_a, w_x, b_x, lru_lambda, w_out, g_ffn,
              w_rg, b_rg, w_re, b_re, w_e_gate, w_e_up, w_e_down, w_ple, w_pg):
    b, s = x_prompt.shape[:2]
    db, dq = x_sample.shape[:2]
    n_pages = page_table.shape[1]
    past_len = n_pages * cache_k.shape[2]
    pos_p = jnp.arange(s)
    pos_s = past_len + jnp.arange(dq)
    yp, ys = x_prompt, x_sample
    kp_l, vp_l, cp_l, hp_l = [], [], [], []
    ks_l, vs_l, cs_l, hs_l = [], [], [], []
    for l in range(DEPTH):
        w = dict(g_mix=g_mix[l], w_in=w_in[l], g_q=g_q[l], g_k=g_k[l],
                 lam_q1=lam_q1[l], lam_k1=lam_k1[l], lam_q2=lam_q2[l], lam_k2=lam_k2[l],
                 g_sub=g_sub[l], conv_w=conv_w[l], conv_b=conv_b[l], w_a=w_a[l], b_a=b_a[l],
                 w_x=w_x[l], b_x=b_x[l], lru_lambda=lru_lambda[l], w_out=w_out[l],
                 g_ffn=g_ffn[l], w_rg=w_rg[l], b_rg=b_rg[l], w_re=w_re[l], b_re=b_re[l],
                 w_e_gate=w_e_gate[l], w_e_up=w_e_up[l], w_e_down=w_e_down[l],
                 w_ple=w_ple[l], w_pg=w_pg[l])
        lam_init = 0.8 - 0.6 * math.exp(-0.3 * l)
        conv0 = jnp.zeros((b, CONV_W - 1, LRU_WIDTH), x_prompt.dtype)
        h0 = jnp.zeros((b, LRU_WIDTH), x_prompt.dtype)
        yp, kp, vp, cp, hp = trunk_layer(yp, p_prompt[l], pos_p, conv0, h0,
                                         prompt_attention, w, lam_init)
        past_k = cache_k[l, page_table].reshape(db, past_len, N_ATT_HEADS, 2, SUB_DIM)
        past_v = cache_v[l, page_table].reshape(db, past_len, N_ATT_HEADS, V_HEAD_DIM)
        attend_s = functools.partial(sample_attention, past_k=past_k, past_v=past_v)
        ys, ksm, vsm, csm, hsm = trunk_layer(ys, p_sample[l], pos_s, state_conv[l], state_h[l],
                                             attend_s, w, lam_init)
        kp_l.append(kp); vp_l.append(vp); cp_l.append(cp); hp_l.append(hp)
        ks_l.append(ksm); vs_l.append(vsm); cs_l.append(csm); hs_l.append(hsm)
    k_prompt = jnp.stack(kp_l); v_prompt = jnp.stack(vp_l)
    conv_prompt = jnp.stack(cp_l); h_prompt = jnp.stack(hp_l)
    k_sample = jnp.stack(ks_l); v_sample = jnp.stack(vs_l)
    conv_sample = jnp.stack(cs_l); h_sample = jnp.stack(hs_l)
    return (yp, ys, k_prompt, v_prompt, conv_prompt, h_prompt,
            k_sample, v_sample, conv_sample, h_sample)
```

```python
import functools
import math

import jax
import jax.numpy as jnp
from jax import lax
from jax.experimental import pallas as pl
from jax.experimental.pallas import tpu as pltpu

F32 = jnp.float32
BF16 = jnp.bfloat16

D_MODEL = 1024
ATT_WIDTH = 512
N_HEADS = 4
HEAD_DIM = 128
SUB_DIM = 64
ROPE_DIM = 16
ROPE_THETA = 500000.0
LRU_WIDTH = 512
LRU_BLOCK = 64
CONV_W = 4
RG_C = 8.0
IN_WIDTH = 3 * ATT_WIDTH + 2 * LRU_WIDTH
N_GROUPS = 4
EXPERTS_PER_GROUP = 4
N_EXPERTS = 16
D_EXPERT = 512
PLE_DIM = 256
EPS = 1e-6
NEG = -1e30

LANES = 128
SUBLANES = 8
VMEM_LIMIT = 56 * 1024 * 1024

TOKEN_TILE = 512
MOE_TILE = 512
ATTN_TILE = 512
LRU_CHUNK = 256
PAGES_PER_CHUNK = 16


def _params(*sem):
    return pltpu.CompilerParams(dimension_semantics=sem, vmem_limit_bytes=VMEM_LIMIT)


def _const_spec(shape):
    nd = len(shape)
    return pl.BlockSpec(shape, lambda *_: (0,) * nd, pipeline_mode=pl.Buffered(1))


def _layer_spec(shape, layer):
    nd = len(shape)
    return pl.BlockSpec((None,) + tuple(shape), lambda *_: (layer,) + (0,) * nd,
                        pipeline_mode=pl.Buffered(1))


def _dot(a, b):
    return jnp.dot(a, b, preferred_element_type=F32)


def _dot_nt(a, b):
    return lax.dot_general(a, b, (((1,), (1,)), ((), ())), preferred_element_type=F32)


def _in_proj_kernel(x_ref, gmix_ref, w_ref, gq_ref, gk_ref, cos_ref, sa_ref, sb_ref,
                    q_ref, k_ref, v_ref, kb_ref, vb_ref, u_ref, g_ref, wbf_ref):
    @pl.when(pl.program_id(0) == 0)
    def _():
        wbf_ref[...] = w_ref[...].astype(BF16)

    x = x_ref[...]
    ms = jnp.mean(x * x, axis=-1, keepdims=True)
    h = (x * lax.rsqrt(ms + EPS) * gmix_ref[...]).astype(BF16)

    lane = lax.broadcasted_iota(jnp.int32, (1, HEAD_DIM), 1)
    lo = lane < SUB_DIM
    cos = cos_ref[...]
    sa = sa_ref[...]
    sb = sb_ref[...]

    def norm_rope(z, g):
        sq = z * z
        tot = jnp.sum(sq, axis=-1, keepdims=True)
        low = jnp.sum(jnp.where(lo, sq, 0.0), axis=-1, keepdims=True)
        ms2 = jnp.where(lo, low, tot - low) * (1.0 / SUB_DIM)
        y = z * lax.rsqrt(ms2 + EPS) * g
        return y * cos + pltpu.roll(y, HEAD_DIM - 8, 1) * sa + pltpu.roll(y, 8, 1) * sb

    zq = _dot(h, wbf_ref[:, 0:ATT_WIDTH])
    gq = gq_ref[...]
    for hd in range(N_HEADS):
        sl = slice(hd * HEAD_DIM, (hd + 1) * HEAD_DIM)
        q_ref[:, sl] = (norm_rope(zq[:, sl], gq) * (SUB_DIM ** -0.5)).astype(BF16)
    zk = _dot(h, wbf_ref[:, ATT_WIDTH:2 * ATT_WIDTH])
    gk = gk_ref[...]
    for hd in range(N_HEADS):
        sl = slice(hd * HEAD_DIM, (hd + 1) * HEAD_DIM)
        kh = norm_rope(zk[:, sl], gk)
        k_ref[:, sl] = kh
        kb_ref[:, sl] = kh.astype(BF16)
    zv = _dot(h, wbf_ref[:, 2 * ATT_WIDTH:3 * ATT_WIDTH])
    v_ref[...] = zv
    vb_ref[...] = zv.astype(BF16)
    u_ref[...] = _dot(h, wbf_ref[:, 3 * ATT_WIDTH:3 * ATT_WIDTH + LRU_WIDTH])
    g_ref[...] = _dot(h, wbf_ref[:, 3 * ATT_WIDTH + LRU_WIDTH:IN_WIDTH])


def _in_proj(x, g_mix, w_in, gq, gk, tabs, tm, tab_blocks, layer):
    t = x.shape[0]
    n = t // tm
    row = lambda w: pl.BlockSpec((tm, w), lambda i: (i, 0))
    tab = pl.BlockSpec((tm, HEAD_DIM), lambda i: (i % tab_blocks, 0))
    outs = [
        jax.ShapeDtypeStruct((t, ATT_WIDTH), BF16),
        jax.ShapeDtypeStruct((t, ATT_WIDTH), F32),
        jax.ShapeDtypeStruct((t, ATT_WIDTH), F32),
        jax.ShapeDtypeStruct((t, ATT_WIDTH), BF16),
        jax.ShapeDtypeStruct((t, ATT_WIDTH), BF16),
        jax.ShapeDtypeStruct((t, LRU_WIDTH), F32),
        jax.ShapeDtypeStruct((t, LRU_WIDTH), F32),
    ]
    return pl.pallas_call(
        _in_proj_kernel,
        grid=(n,),
        in_specs=[row(D_MODEL), _const_spec((1, D_MODEL)), _layer_spec((D_MODEL, IN_WIDTH), layer),
                  _const_spec((1, HEAD_DIM)), _const_spec((1, HEAD_DIM)), tab, tab, tab],
        out_specs=[row(ATT_WIDTH)] * 5 + [row(LRU_WIDTH)] * 2,
        out_shape=outs,
        scratch_shapes=[pltpu.VMEM((D_MODEL, IN_WIDTH), BF16)],
        compiler_params=_params("arbitrary"),
        name="in_proj",
    )(x, g_mix, w_in, gq, gk, *tabs)


def _lambda_value(lamp_ref, lam_init):
    lp = lamp_ref[...]
    s1 = jnp.sum(lp[0:1] * lp[1:2], axis=-1, keepdims=True)
    s2 = jnp.sum(lp[2:3] * lp[3:4], axis=-1, keepdims=True)
    return jnp.exp(s1) - jnp.exp(s2) + lam_init


def _attn_kernel(lamp_ref, gsub_ref, q_ref, k_ref, v_ref, o_ref, *, lam_init, blk):
    i = pl.program_id(1)
    lam = _lambda_value(lamp_ref, lam_init)
    lane = lax.broadcasted_iota(jnp.int32, (1, HEAD_DIM), 1)
    lo = lane < SUB_DIM
    row = lax.broadcasted_iota(jnp.int32, (blk, blk), 0)
    col = lax.broadcasted_iota(jnp.int32, (blk, blk), 1)
    causal = col <= row
    gsub = gsub_ref[...]
    zero = jnp.zeros((), BF16)

    for hd in range(N_HEADS):
        sl = slice(hd * HEAD_DIM, (hd + 1) * HEAD_DIM)
        q = q_ref[:, sl]
        qs = (jnp.where(lo, q, zero), jnp.where(lo, zero, q))

        def step(j, carry, masked):
            start = pl.multiple_of(j * blk, blk)
            kb = k_ref[pl.ds(start, blk), sl]
            vb = v_ref[pl.ds(start, blk), sl]
            out = []
            for c in range(2):
                m, l, acc = carry[c]
                s = _dot_nt(qs[c], kb)
                if masked:
                    s = jnp.where(causal, s, NEG)
                m_new = jnp.maximum(m, jnp.max(s, axis=-1, keepdims=True))
                p = jnp.exp(s - m_new)
                alpha = jnp.exp(m - m_new)
                l = alpha * l + jnp.sum(p, axis=-1, keepdims=True)
                acc = alpha * acc + _dot(p.astype(BF16), vb)
                out.append((m_new, l, acc))
            return tuple(out)

        init = tuple((jnp.full((blk, 1), NEG, F32), jnp.zeros((blk, 1), F32),
                      jnp.zeros((blk, HEAD_DIM), F32)) for _ in range(2))
        carry = lax.fori_loop(0, i, functools.partial(step, masked=False), init)
        (_, l0, a0), (_, l1, a1) = step(i, carry, True)
        o = a0 / l0 - lam * (a1 / l1)
        ms = jnp.mean(o * o, axis=-1, keepdims=True)
        o = o * lax.rsqrt(ms + EPS) * gsub * (1.0 - lam_init)
        o_ref[:, sl] = o.astype(BF16)


def _prompt_attention(q, kb, vb, lamp, gsub, lam_init, batch, seq):
    blk = ATTN_TILE
    nq = seq // blk
    t = batch * seq
    return pl.pallas_call(
        functools.partial(_attn_kernel, lam_init=lam_init, blk=blk),
        grid=(batch, nq),
        in_specs=[pl.BlockSpec((4, SUB_DIM), lambda b, i: (0, 0)),
                  pl.BlockSpec((1, HEAD_DIM), lambda b, i: (0, 0)),
                  pl.BlockSpec((blk, ATT_WIDTH), lambda b, i: (b * nq + i, 0)),
                  pl.BlockSpec((seq, ATT_WIDTH), lambda b, i: (b, 0)),
                  pl.BlockSpec((seq, ATT_WIDTH), lambda b, i: (b, 0))],
        out_specs=pl.BlockSpec((blk, ATT_WIDTH), lambda b, i: (b * nq + i, 0)),
        out_shape=jax.ShapeDtypeStruct((t, ATT_WIDTH), BF16),
        compiler_params=_params("arbitrary", "arbitrary"),
        name="prompt_attention",
    )(lamp, gsub, q, kb, vb)


def _lru_gates(uc, wa_ref, wx_ref, ba, bx, sp):
    ub = uc.astype(BF16)
    half = LRU_WIDTH // 2
    ra = jnp.concatenate([_dot(ub[:, :half], wa_ref[0]), _dot(ub[:, half:], wa_ref[1])], axis=-1) + ba
    rx = jnp.concatenate([_dot(ub[:, :half], wx_ref[0]), _dot(ub[:, half:], wx_ref[1])], axis=-1) + bx
    r = jax.nn.sigmoid(ra)
    gate_i = jax.nn.sigmoid(rx)
    log_a = -RG_C * r * sp
    a = jnp.exp(log_a)
    one_minus_a2 = -jnp.tanh(log_a) * (a * a + 1.0)
    u = jnp.sqrt(one_minus_a2) * (gate_i * uc)
    return a, u


def _softplus(x):
    return jnp.maximum(x, 0.0) + jnp.log1p(jnp.exp(-jnp.abs(x)))


def _lru_kernel(zu_ref, zg_ref, cw_ref, cb_ref, wa_ref, wx_ref, ba_ref, bx_ref, lam_ref,
                r_ref, conv_ref, hlast_ref, xpad, a_s, u_s, hs_s, h_state, *, tc, nb):
    c = pl.program_id(0)
    pad = SUBLANES
    n_slab = LRU_WIDTH // LANES

    @pl.when(c == 0)
    def _():
        xpad[:, 0:pad, :] = jnp.zeros((nb, pad, LRU_WIDTH), F32)
        h_state[...] = jnp.zeros((n_slab, nb, LANES), F32)

    xpad[:, pad:pad + tc, :] = zu_ref[...]
    sp = _softplus(-lam_ref[...])
    cw = cw_ref[...]
    cb = cb_ref[...]
    ba = ba_ref[...]
    bx = bx_ref[...]
    for b in range(nb):
        uc = cb
        for j in range(CONV_W):
            off = pad - (CONV_W - 1) + j
            uc = uc + xpad[b, off:off + tc, :] * cw[j:j + 1, :]
        a, u = _lru_gates(uc, wa_ref, wx_ref, ba, bx, sp)
        for j in range(n_slab):
            a_s[j, pl.ds(b, tc, stride=nb), :] = a[:, j * LANES:(j + 1) * LANES]
            u_s[j, pl.ds(b, tc, stride=nb), :] = u[:, j * LANES:(j + 1) * LANES]

    def step(t, h):
        off = pl.multiple_of(t * nb, nb)
        h = a_s[:, pl.ds(off, nb), :] * h + u_s[:, pl.ds(off, nb), :]
        hs_s[:, pl.ds(off, nb), :] = h
        return h

    h = lax.fori_loop(0, tc, step, h_state[...], unroll=8)
    h_state[...] = h
    for b in range(nb):
        hs = jnp.concatenate([hs_s[j, pl.ds(b, tc, stride=nb), :] for j in range(n_slab)], axis=-1)
        r_ref[b] = (hs * jax.nn.gelu(zg_ref[b])).astype(BF16)
    tail = xpad[:, tc + pad - (CONV_W - 1):tc + pad, :]
    xpad[:, pad - (CONV_W - 1):pad, :] = tail

    @pl.when(c == pl.num_programs(0) - 1)
    def _():
        conv_ref[...] = tail
        for j in range(n_slab):
            hlast_ref[:, j * LANES:(j + 1) * LANES] = h[j]


def _prompt_lru(zu, zg, cw, cb, wa, wx, ba, bx, lam):
    nb, seq, _ = zu.shape
    tc = LRU_CHUNK
    blk = pl.BlockSpec((nb, tc, LRU_WIDTH), lambda c: (0, c, 0))
    return pl.pallas_call(
        functools.partial(_lru_kernel, tc=tc, nb=nb),
        grid=(seq // tc,),
        in_specs=[blk, blk, _const_spec((CONV_W, LRU_WIDTH)), _const_spec((1, LRU_WIDTH)),
                  _const_spec((2, 256, 256)), _const_spec((2, 256, 256)),
                  _const_spec((1, LRU_WIDTH)), _const_spec((1, LRU_WIDTH)), _const_spec((1, LRU_WIDTH))],
        out_specs=[blk,
                   pl.BlockSpec((nb, CONV_W - 1, LRU_WIDTH), lambda c: (0, 0, 0)),
                   pl.BlockSpec((nb, LRU_WIDTH), lambda c: (0, 0))],
        out_shape=[jax.ShapeDtypeStruct((nb, seq, LRU_WIDTH), BF16),
                   jax.ShapeDtypeStruct((nb, CONV_W - 1, LRU_WIDTH), F32),
                   jax.ShapeDtypeStruct((nb, LRU_WIDTH), F32)],
        scratch_shapes=[pltpu.VMEM((nb, tc + SUBLANES, LRU_WIDTH), F32),
                        pltpu.VMEM((LRU_WIDTH // LANES, tc * nb, LANES), F32),
                        pltpu.VMEM((LRU_WIDTH // LANES, tc * nb, LANES), F32),
                        pltpu.VMEM((LRU_WIDTH // LANES, tc * nb, LANES), F32),
                        pltpu.VMEM((LRU_WIDTH // LANES, nb, LANES), F32)],
        compiler_params=_params("arbitrary"),
        name="prompt_lru",
    )(zu, zg, cw, cb, wa, wx, ba, bx, lam)


def _lru_step_kernel(zu_ref, zg_ref, sc_ref, h0_ref, cw_ref, cb_ref, wa_ref, wx_ref,
                     ba_ref, bx_ref, lam_ref, r_ref, conv_ref, h_ref):
    zu = zu_ref[...]
    cw = cw_ref[...]
    uc = cb_ref[...] + zu * cw[CONV_W - 1:CONV_W, :]
    for j in range(CONV_W - 1):
        uc = uc + sc_ref[j] * cw[j:j + 1, :]
    sp = _softplus(-lam_ref[...])
    a, u = _lru_gates(uc, wa_ref, wx_ref, ba_ref[...], bx_ref[...], sp)
    h = a * h0_ref[...] + u
    h_ref[...] = h
    r_ref[...] = (h * jax.nn.gelu(zg_ref[...])).astype(BF16)
    conv_ref[0] = sc_ref[1]
    conv_ref[1] = sc_ref[2]
    conv_ref[2] = zu


def _sample_lru(zu, zg, sc, h0, cw, cb, wa, wx, ba, bx, lam):
    n = zu.shape[0]
    return pl.pallas_call(
        _lru_step_kernel,
        out_shape=[jax.ShapeDtypeStruct((n, LRU_WIDTH), BF16),
                   jax.ShapeDtypeStruct((CONV_W - 1, n, LRU_WIDTH), F32),
                   jax.ShapeDtypeStruct((n, LRU_WIDTH), F32)],
        name="sample_lru",
    )(zu, zg, sc, h0, cw, cb, wa, wx, ba, bx, lam)


def _split_bf16(x):
    hi = x.astype(BF16)
    lo = (x - hi.astype(F32)).astype(BF16)
    return hi, lo


def _route(logits):
    gl = [logits[g:g + 1] for g in range(N_GROUPS)]
    gmax = functools.reduce(jnp.maximum, gl)
    gidx = jnp.where(gl[0] == gmax, 0, jnp.where(gl[1] == gmax, 1, jnp.where(gl[2] == gmax, 2, 3)))
    gden = functools.reduce(lambda a, b: a + b, [jnp.exp(g - gmax) for g in gl])
    g_w = 1.0 / gden
    el = []
    for k in range(EXPERTS_PER_GROUP):
        v = logits[4 + k:5 + k]
        for g in range(1, N_GROUPS):
            v = jnp.where(gidx == g, logits[4 + 4 * g + k:5 + 4 * g + k], v)
        el.append(v)
    emax = functools.reduce(jnp.maximum, el)
    ex = [jnp.exp(e - emax) for e in el]
    eden = functools.reduce(lambda a, b: a + b, ex)
    ep = [e / eden for e in ex]
    w1 = functools.reduce(jnp.maximum, ep)
    i1 = jnp.where(ep[0] == w1, 0, jnp.where(ep[1] == w1, 1, jnp.where(ep[2] == w1, 2, 3)))
    rest = [jnp.where(i1 == k, -1.0, ep[k]) for k in range(EXPERTS_PER_GROUP)]
    w2 = functools.reduce(jnp.maximum, rest)
    i2 = jnp.where(rest[0] == w2, 0, jnp.where(rest[1] == w2, 1, jnp.where(rest[2] == w2, 2, 3)))
    tot = w1 + w2
    gate1 = g_w * (w1 / tot)
    gate2 = g_w * (w2 / tot)
    rows = []
    for g in range(N_GROUPS):
        for k in range(EXPERTS_PER_GROUP):
            w = jnp.where(i1 == k, gate1, jnp.where(i2 == k, gate2, 0.0))
            rows.append(jnp.where(gidx == g, w, 0.0))
    return rows


def _out_proj_kernel(x_ref, o_ref, r_ref, w_ref, gffn_ref, wr_ref, br_ref,
                     x1_ref, h2_ref, comb_ref, wbf_ref):
    @pl.when(pl.program_id(0) == 0)
    def _():
        wbf_ref[...] = w_ref[...].astype(BF16)

    x1 = (x_ref[...] + _dot(o_ref[...], wbf_ref[0:ATT_WIDTH, :])
          + _dot(r_ref[...], wbf_ref[ATT_WIDTH:, :]))
    x1_ref[...] = x1
    ms = jnp.mean(x1 * x1, axis=-1, keepdims=True)
    h2 = x1 * lax.rsqrt(ms + EPS) * gffn_ref[...]
    h_hi, h_lo = _split_bf16(h2)
    h2_ref[...] = h_hi
    w_hi, w_lo = _split_bf16(wr_ref[...])
    logits = (_dot_nt(w_hi, h_hi) + _dot_nt(w_hi, h_lo) + _dot_nt(w_lo, h_hi)) + br_ref[...]
    rows = _route(logits)
    for e in range(N_EXPERTS):
        comb_ref[e:e + 1, :] = rows[e]


def _out_proj(x, o, r, w_out, g_ffn, wr, br, tm, layer):
    t = x.shape[0]
    n = t // tm
    row = lambda w: pl.BlockSpec((tm, w), lambda i: (i, 0))
    return pl.pallas_call(
        _out_proj_kernel,
        grid=(n,),
        in_specs=[row(D_MODEL), row(ATT_WIDTH), row(LRU_WIDTH), _layer_spec((D_MODEL, D_MODEL), layer),
                  _const_spec((1, D_MODEL)), _const_spec((32, D_MODEL)), _const_spec((32, 1))],
        out_specs=[row(D_MODEL), row(D_MODEL), pl.BlockSpec((N_EXPERTS, tm), lambda i: (0, i))],
        out_shape=[jax.ShapeDtypeStruct((t, D_MODEL), F32),
                   jax.ShapeDtypeStruct((t, D_MODEL), BF16),
                   jax.ShapeDtypeStruct((N_EXPERTS, t), F32)],
        scratch_shapes=[pltpu.VMEM((D_MODEL, D_MODEL), BF16)],
        compiler_params=_params("arbitrary"),
        name="out_proj_router",
    )(x, o, r, w_out, g_ffn, wr, br)


def _moe_kernel(x1_ref, h2_ref, comb_ref, wg_ref, wu_ref, wd_ref, p_ref, wpg_ref, wple_ref,
                y_ref, acc_ref, wpg_bf, wple_bf):
    i = pl.program_id(0)
    e = pl.program_id(1)

    @pl.when((i == 0) & (e == 0))
    def _():
        wpg_bf[...] = wpg_ref[...].astype(BF16)
        wple_bf[...] = wple_ref[...].astype(BF16)

    @pl.when(e == 0)
    def _():
        acc_ref[...] = jnp.zeros_like(acc_ref)

    h = h2_ref[...]
    he = jax.nn.silu(_dot(h, wg_ref[...].astype(BF16))) * _dot(h, wu_ref[...].astype(BF16))
    acc_ref[...] += comb_ref[0] * _dot(he.astype(BF16), wd_ref[...].astype(BF16))

    @pl.when(e == pl.num_programs(1) - 1)
    def _():
        x2 = x1_ref[...] + acc_ref[...]
        gate = jax.nn.sigmoid(_dot(x2.astype(BF16), wpg_bf[...]))
        ple = _dot(p_ref[...].astype(BF16), wple_bf[...])
        y_ref[...] = x2 + gate * ple


def _moe(x1, h2, comb, wg, wu, wd, p, w_pg, w_ple, tm, layer):
    t = x1.shape[0]
    n = t // tm
    row = lambda w: pl.BlockSpec((tm, w), lambda i, e: (i, 0))
    expert = lambda a, b: pl.BlockSpec((None, None, a, b), lambda i, e: (layer, e, 0, 0))
    return pl.pallas_call(
        _moe_kernel,
        grid=(n, N_EXPERTS),
        in_specs=[row(D_MODEL), row(D_MODEL),
                  pl.BlockSpec((1, tm, 1), lambda i, e: (e, i, 0)),
                  expert(D_MODEL, D_EXPERT), expert(D_MODEL, D_EXPERT), expert(D_EXPERT, D_MODEL),
                  row(PLE_DIM), _layer_spec((D_MODEL, D_MODEL), layer),
                  _layer_spec((PLE_DIM, D_MODEL), layer)],
        out_specs=row(D_MODEL),
        out_shape=jax.ShapeDtypeStruct((t, D_MODEL), F32),
        scratch_shapes=[pltpu.VMEM((tm, D_MODEL), F32), pltpu.VMEM((D_MODEL, D_MODEL), BF16),
                        pltpu.VMEM((PLE_DIM, D_MODEL), BF16)],
        compiler_params=_params("arbitrary", "arbitrary"),
        name="moe_ple",
    )(x1, h2, comb, wg, wu, wd, p, w_pg, w_ple)


def _paged_attn_kernel(pt_ref, lamp_ref, gsub_ref, q_ref, kn_ref, vn_ref, ck_ref, cv_ref,
                       o_ref, kbuf, vbuf, sem, *, layer, lam_init, n_seq, n_pages, ppc):
    n_chunks = n_pages // ppc
    total = n_seq * n_chunks
    rows = 2 * N_HEADS
    lam = _lambda_value(lamp_ref, lam_init)

    def copies(it, slot):
        b = it // n_chunks
        c = it % n_chunks
        out = []
        for p in range(ppc):
            page = pt_ref[b, c * ppc + p]
            out.append(pltpu.make_async_copy(ck_ref.at[layer, page], kbuf.at[slot, p], sem.at[0, slot]))
            out.append(pltpu.make_async_copy(cv_ref.at[layer, page], vbuf.at[slot, p], sem.at[1, slot]))
        return out

    def start(it, slot):
        for cp in copies(it, slot):
            cp.start()

    def wait(it, slot):
        for cp in copies(it, slot):
            cp.wait()

    sub = lax.broadcasted_iota(jnp.int32, (rows, ATT_WIDTH), 0)
    lane = lax.broadcasted_iota(jnp.int32, (rows, ATT_WIDTH), 1)
    own = (lane // SUB_DIM) == sub
    same_head = (lane // HEAD_DIM) == (sub // 2)
    coef = jnp.where(sub % 2 == 0, 1.0, -lam)
    gsub = gsub_ref[...]

    start(0, 0)

    def seq_body(b, _):
        qrow = q_ref[pl.ds(b, 1), :]
        qb = jnp.where(own, qrow, 0.0)
        qbf = qb.astype(BF16)

        def chunk_body(c, carry):
            m, l, acc = carry
            it = b * n_chunks + c
            slot = it % 2

            @pl.when(it + 1 < total)
            def _():
                start(it + 1, 1 - slot)

            wait(it, slot)
            kc = kbuf[slot].reshape(ppc * 128, ATT_WIDTH).astype(BF16)
            vc = vbuf[slot].reshape(ppc * 128, ATT_WIDTH).astype(BF16)
            s = _dot_nt(qbf, kc)
            m_new = jnp.maximum(m, jnp.max(s, axis=-1, keepdims=True))
            p = jnp.exp(s - m_new)
            alpha = jnp.exp(m - m_new)
            l = alpha * l + jnp.sum(p, axis=-1, keepdims=True)
            acc = alpha * acc + _dot(p.astype(BF16), vc)
            return m_new, l, acc

        init = (jnp.full((rows, 1), NEG, F32), jnp.zeros((rows, 1), F32),
                jnp.zeros((rows, ATT_WIDTH), F32))
        m, l, acc = lax.fori_loop(0, n_chunks, chunk_body, init)
        s_self = jnp.sum(qb * kn_ref[pl.ds(b, 1), :], axis=-1, keepdims=True)
        m_new = jnp.maximum(m, s_self)
        p_self = jnp.exp(s_self - m_new)
        alpha = jnp.exp(m - m_new)
        l = alpha * l + p_self
        acc = alpha * acc + p_self * vn_ref[pl.ds(b, 1), :]
        o8 = jnp.where(same_head, acc / l * coef, 0.0)
        o = jnp.sum(o8, axis=0, keepdims=True)
        parts = []
        for hd in range(N_HEADS):
            oh = o[:, hd * HEAD_DIM:(hd + 1) * HEAD_DIM]
            ms = jnp.mean(oh * oh, axis=-1, keepdims=True)
            parts.append(oh * lax.rsqrt(ms + EPS) * gsub * (1.0 - lam_init))
        o_ref[pl.ds(b, 1), :] = jnp.concatenate(parts, axis=-1)
        return 0

    lax.fori_loop(0, n_seq, seq_body, 0)


def _sample_attention(q, k_new, v_new, cache_k, cache_v, page_table, lamp, gsub, layer, lam_init):
    n_seq, n_pages = page_table.shape
    ppc = PAGES_PER_CHUNK
    page = cache_k.shape[2]
    vm = lambda: pl.BlockSpec(memory_space=pltpu.VMEM)
    return pl.pallas_call(
        functools.partial(_paged_attn_kernel, layer=layer, lam_init=lam_init, n_seq=n_seq,
                          n_pages=n_pages, ppc=ppc),
        grid_spec=pltpu.PrefetchScalarGridSpec(
            num_scalar_prefetch=1,
            grid=(1,),
            in_specs=[vm(), vm(), vm(), vm(), vm(),
                      pl.BlockSpec(memory_space=pl.ANY), pl.BlockSpec(memory_space=pl.ANY)],
            out_specs=vm(),
            scratch_shapes=[pltpu.VMEM((2, ppc, page, ATT_WIDTH), F32),
                            pltpu.VMEM((2, ppc, page, ATT_WIDTH), F32),
                            pltpu.SemaphoreType.DMA((2, 2))]),
        out_shape=jax.ShapeDtypeStruct((n_seq, ATT_WIDTH), F32),
        compiler_params=_params("arbitrary"),
        name="sample_paged_attention",
    )(page_table, lamp, gsub, q, k_new, v_new, cache_k, cache_v)


def _rope_tables(pos):
    half = ROPE_DIM // 2
    n = pos.shape[0]
    inv = ROPE_THETA ** (-jnp.arange(half, dtype=F32) * 2.0 / ROPE_DIM)
    ang = pos.astype(F32)[:, None] * inv[None, :]
    cos = jnp.cos(ang)
    sin = jnp.sin(ang)
    rest = SUB_DIM - ROPE_DIM
    z8 = jnp.zeros((n, half), F32)
    zr = jnp.zeros((n, rest), F32)
    c = jnp.concatenate([cos, cos, jnp.ones((n, rest), F32)], axis=-1)
    sa = jnp.concatenate([-sin, z8, zr], axis=-1)
    sb = jnp.concatenate([z8, sin, zr], axis=-1)
    return tuple(jnp.tile(t, (1, 2)) for t in (c, sa, sb))


def _block_diag(w):
    w4 = w.reshape(2, 4, LRU_BLOCK, LRU_BLOCK)
    eye = jnp.eye(4, dtype=w.dtype)
    bd = jnp.einsum('hncd,nm->hncmd', w4, eye).reshape(2, 4 * LRU_BLOCK, 4 * LRU_BLOCK)
    return bd.astype(BF16)


def kernel(x_prompt, x_sample, cache_k, cache_v, state_conv, state_h, page_table, p_prompt, p_sample, g_mix, w_in, g_q, g_k, lam_q1, lam_k1, lam_q2, lam_k2, g_sub, conv_w, conv_b, w_a, b_a, w_x, b_x, lru_lambda, w_out, g_ffn, w_rg, b_rg, w_re, b_re, w_e_gate, w_e_up, w_e_down, w_ple, w_pg):
    batch, seq, _ = x_prompt.shape
    n_seq = x_sample.shape[0]
    depth = w_in.shape[0]
    n_pool, page = cache_k.shape[1], cache_k.shape[2]
    past_len = page_table.shape[1] * page
    t = batch * seq

    ck = cache_k.reshape(depth, n_pool, page, ATT_WIDTH)
    cv = cache_v.reshape(depth, n_pool, page, ATT_WIDTH)
    tabs_p = _rope_tables(jnp.arange(seq))
    tabs_s = _rope_tables(jnp.full((n_seq,), past_len, jnp.int32))

    xp = x_prompt.reshape(t, D_MODEL)
    xs = x_sample.reshape(n_seq, D_MODEL)
    kp_l, vp_l, cp_l, hp_l, ks_l, vs_l, cs_l, hs_l = ([] for _ in range(8))
    for l in range(depth):
        lam_init = 0.8 - 0.6 * math.exp(-0.3 * l)
        row = lambda a: a[l].reshape(1, -1)
        gq = jnp.tile(row(g_q), (1, 2))
        gk = jnp.tile(row(g_k), (1, 2))
        lamp = jnp.stack([lam_q1[l], lam_k1[l], lam_q2[l], lam_k2[l]])
        gsub = row(g_sub)
        wa = _block_diag(w_a[l])
        wx = _block_diag(w_x[l])
        wr = jnp.zeros((32, D_MODEL), F32).at[0:N_GROUPS].set(w_rg[l].T).at[4:4 + N_EXPERTS].set(w_re[l].T)
        br = jnp.zeros((32, 1), F32).at[0:N_GROUPS, 0].set(b_rg[l]).at[4:4 + N_EXPERTS, 0].set(b_re[l])
        lru_w = (conv_w[l], row(conv_b), wa, wx, row(b_a), row(b_x), row(lru_lambda))

        q, k, v, kb, vb, zu, zg = _in_proj(xp, row(g_mix), w_in, gq, gk, tabs_p,
                                           TOKEN_TILE, seq // TOKEN_TILE, l)
        o = _prompt_attention(q, kb, vb, lamp, gsub, lam_init, batch, seq)
        r, conv_new, h_last = _prompt_lru(zu.reshape(batch, seq, LRU_WIDTH),
                                          zg.reshape(batch, seq, LRU_WIDTH), *lru_w)
        x1, h2, comb = _out_proj(xp, o, r.reshape(t, LRU_WIDTH), w_out, row(g_ffn), wr, br,
                                 TOKEN_TILE, l)
        xp = _moe(x1, h2, comb.reshape(N_EXPERTS, t, 1), w_e_gate, w_e_up, w_e_down,
                  p_prompt[l].reshape(t, PLE_DIM), w_pg, w_ple, MOE_TILE, l)
        kp_l.append(k.reshape(batch, seq, N_HEADS, HEAD_DIM))
        vp_l.append(v.reshape(batch, seq, N_HEADS, HEAD_DIM))
        cp_l.append(conv_new)
        hp_l.append(h_last)

        qs, ksm, vsm, _, _, zus, zgs = _in_proj(xs, row(g_mix), w_in, gq, gk, tabs_s, n_seq, 1, l)
        os_ = _sample_attention(qs.astype(F32), ksm, vsm, ck, cv, page_table, lamp, gsub, l, lam_init)
        rs, conv_s, h_s = _sample_lru(zus, zgs, jnp.swapaxes(state_conv[l], 0, 1), state_h[l], *lru_w)
        x1s, h2s, combs = _out_proj(xs, os_.astype(BF16), rs, w_out, row(g_ffn), wr, br, n_seq, l)
        xs = _moe(x1s, h2s, combs.reshape(N_EXPERTS, n_seq, 1), w_e_gate, w_e_up, w_e_down,
                  p_sample[l].reshape(n_seq, PLE_DIM), w_pg, w_ple, n_seq, l)
        ks_l.append(ksm.reshape(n_seq, 1, N_HEADS, HEAD_DIM))
        vs_l.append(vsm.reshape(n_seq, 1, N_HEADS, HEAD_DIM))
        cs_l.append(jnp.swapaxes(conv_s, 0, 1))
        hs_l.append(h_s)

    return (xp.reshape(batch, seq, D_MODEL), xs.reshape(n_seq, 1, D_MODEL),
            jnp.stack(kp_l), jnp.stack(vp_l), jnp.stack(cp_l), jnp.stack(hp_l),
            jnp.stack(ks_l), jnp.stack(vs_l), jnp.stack(cs_l), jnp.stack(hs_l))
```
